```python
import math
import jax, jax.numpy as jnp
from jax import lax
import numpy as np

D_MODEL = 2048
BATCH = 4
SEQ = 2048
DEPTH = 4
DEC_BATCH = 128
DEC_SEQ = 8
PAST_LEN = 16384
PAGE_SIZE = 128

D_CONF = D_MODEL // 2
CONF_WIDTH = 31
D_SHORT = D_MODEL // 2
SHORT_WIDTH = 3
D_SSM = D_MODEL
SSM_HEAD_DIM = 64
SSM_HEADS = D_SSM // SSM_HEAD_DIM
SSM_GROUPS = 4
SSM_STATE = 128
SSM_CONV_WIDTH = 4
SSD_CHUNK = 128
D_XBC = D_SSM + 2 * SSM_GROUPS * SSM_STATE
N_BRANCH = 3
D_FF = 5504
ALPHA = (2 * DEPTH) ** 0.25
BETA = (8 * DEPTH) ** -0.25
LN_EPS = 1e-5
RMS_EPS = 1e-5
D_IN = 2 * D_CONF + 3 * D_SHORT + D_SSM + D_XBC + SSM_HEADS + N_BRANCH * D_MODEL

kernel_name = 'hybrid_conformer_shortconv_ssd_decoder_step'


def layer_norm(x, g, b):
    xf = x.astype(jnp.float32)
    mu = jnp.mean(xf, axis=-1, keepdims=True)
    var = jnp.mean(jnp.square(xf - mu), axis=-1, keepdims=True)
    return ((xf - mu) * lax.rsqrt(var + LN_EPS) * g + b).astype(x.dtype)


def swiglu(x, w_up, w_down):
    gate, up = jnp.split(x @ w_up, 2, axis=-1)
    return (jax.nn.silu(gate) * up) @ w_down


def causal_dwconv(u, buf, w):
    full = jnp.concatenate([buf.astype(u.dtype), u], axis=1)
    y = lax.conv_general_dilated(full, w[:, None, :].astype(u.dtype), window_strides=(1,),
                                 padding='VALID', dimension_numbers=('NWC', 'WIO', 'NWC'),
                                 feature_group_count=u.shape[-1])
    new_buf = full[:, full.shape[1] - (w.shape[0] - 1):, :]
    return y, new_buf


def ssd(x, dt, A, Bm, Cm, init_state):
    b, l, h, p = x.shape
    g, n = Bm.shape[2], Bm.shape[3]
    hg = h // g
    q = SSD_CHUNK if l % SSD_CHUNK == 0 else l
    c = l // q
    f32 = jnp.float32
    xc = x.astype(f32).reshape(b, c, q, g, hg, p)
    dtc = dt.astype(f32).reshape(b, c, q, g, hg)
    Bc = Bm.astype(f32).reshape(b, c, q, g, n)
    Cc = Cm.astype(f32).reshape(b, c, q, g, n)
    acum = jnp.cumsum(dtc * A.astype(f32).reshape(g, hg), axis=2)
    xdt = xc * dtc[..., None]
    seg = acum[:, :, :, None] - acum[:, :, None, :]
    mask = jnp.tril(jnp.ones((q, q), dtype=bool))[None, None, :, :, None, None]
    lmat = jnp.exp(jnp.where(mask, seg, -jnp.inf))
    cb = jnp.einsum('bcign,bcjgn->bcijg', Cc, Bc)
    y_diag = jnp.einsum('bcijg,bcijgh,bcjghp->bcighp', cb, lmat, xdt)
    decay_end = jnp.exp(acum[:, :, -1:] - acum)
    chunk_states = jnp.einsum('bcjgn,bcjgh,bcjghp->bcghpn', Bc, decay_end, xdt)
    chunk_decay = jnp.exp(acum[:, :, -1])
    s0 = init_state.astype(f32).reshape(b, g, hg, p, n)

    def step(s, inp):
        dec, st = inp
        return dec[..., None, None] * s + st, s

    s_final, s_in = lax.scan(step, s0, (jnp.moveaxis(chunk_decay, 1, 0), jnp.moveaxis(chunk_states, 1, 0)))
    s_in = jnp.moveaxis(s_in, 0, 1)
    y_off = jnp.einsum('bcign,bcghpn,bcigh->bcighp', Cc, s_in, jnp.exp(acum))
    y = (y_diag + y_off).reshape(b, l, h, p).astype(x.dtype)
    return y, s_final.reshape(b, h, p, n).astype(init_state.dtype)


def gated_group_rmsnorm(y, z, w):
    h = (y * jax.nn.silu(z)).astype(jnp.float32)
    hs = h.reshape(h.shape[:-1] + (SSM_GROUPS, D_SSM // SSM_GROUPS))
    hs = hs * lax.rsqrt(jnp.mean(jnp.square(hs), axis=-1, keepdims=True) + RMS_EPS)
    return (hs.reshape(h.shape) * w).astype(y.dtype)


def token_mixers(x, st_a, st_b, st_c, st_ssm, p):
    bsz, l = x.shape[0], x.shape[1]
    sizes = [D_CONF, D_CONF, D_SHORT, D_SHORT, D_SHORT, D_SSM, D_XBC, SSM_HEADS, N_BRANCH * D_MODEL]
    idx = [int(i) for i in np.cumsum(sizes)[:-1]]
    proj = jnp.einsum('bld,de->ble', x, p['w_in'])
    pa, pa_gate, bg, cg, v, z, xbc, dt_raw, gate_logits = jnp.split(proj, idx, axis=-1)
    a = pa * jax.nn.sigmoid(pa_gate)
    a_conv, buf_a = causal_dwconv(a, st_a, p['conv_a_w'])
    a_conv = a_conv + p['conv_a_b']
    y_a = jax.nn.silu(layer_norm(a_conv, p['ln_a_g'], p['ln_a_b'])) @ p['w_out_a']
    u_conv, buf_b = causal_dwconv(cg * v, st_b, p['conv_b_w'])
    y_b = (bg * u_conv) @ p['w_out_b']
    xbc_conv, buf_c = causal_dwconv(xbc, st_c, p['conv_ssm_w'])
    xbc_act = jax.nn.silu(xbc_conv + p['conv_ssm_b'])
    xs, bm, cm = jnp.split(xbc_act, [D_SSM, D_SSM + SSM_GROUPS * SSM_STATE], axis=-1)
    xh = xs.reshape(bsz, l, SSM_HEADS, SSM_HEAD_DIM)
    dt = jax.nn.softplus(dt_raw.astype(jnp.float32) + p['dt_bias'])
    A = -jnp.exp(p['a_log'].astype(jnp.float32))
    y_s, new_ssm = ssd(xh, dt, A, bm.reshape(bsz, l, SSM_GROUPS, SSM_STATE),
                       cm.reshape(bsz, l, SSM_GROUPS, SSM_STATE), st_ssm)
    y_s = y_s + p['d_skip'][:, None] * xh
    y_c = gated_group_rmsnorm(y_s.reshape(bsz, l, D_SSM), z, p['ssm_norm_w']) @ p['w_out_ssm']
    gates = jax.nn.sigmoid(gate_logits + p['b_gate']).reshape(bsz, l, N_BRANCH, D_MODEL)
    merged = gates[:, :, 0] * y_a + gates[:, :, 1] * y_b + gates[:, :, 2] * y_c
    return merged @ p['w_o'], (buf_a, buf_b, buf_c, new_ssm)


def run_trunk(x, conv_a, conv_b, conv_ssm, ssm, w):
    new_a, new_b, new_c, new_s = [], [], [], []
    for i in range(DEPTH):
        p = {k: arr[i] for k, arr in w.items()}
        x = layer_norm(ALPHA * x + 0.5 * swiglu(x, p['ffn1_up'], p['ffn1_down']), p['ln_g'][0], p['ln_b'][0])
        m, (ba, bb, bc, s) = token_mixers(x, conv_a[i], conv_b[i], conv_ssm[i], ssm[i], p)
        x = layer_norm(ALPHA * x + m, p['ln_g'][1], p['ln_b'][1])
        x = layer_norm(ALPHA * x + 0.5 * swiglu(x, p['ffn2_up'], p['ffn2_down']), p['ln_g'][2], p['ln_b'][2])
        new_a.append(ba)
        new_b.append(bb)
        new_c.append(bc)
        new_s.append(s)
    return x, jnp.stack(new_a), jnp.stack(new_b), jnp.stack(new_c), jnp.stack(new_s)


def setup_inputs(seed: int = 0) -> dict:
    key = jax.random.key(seed)
    ks = jax.random.split(key, 30)
    f32 = jnp.float32

    def nrm(k, shape, s):
        return jax.random.normal(k, shape, f32) * s

    dt0 = jnp.exp(jax.random.uniform(ks[18], (DEPTH, SSM_HEADS), f32, math.log(1e-3), math.log(1e-1)))
    return {
        'x_prompt': nrm(ks[0], (BATCH, SEQ, D_MODEL), 1.0),
        'x_sample': nrm(ks[1], (DEC_BATCH, DEC_SEQ, D_MODEL), 1.0),
        'state_conv_a': nrm(ks[2], (DEPTH, DEC_BATCH, CONF_WIDTH - 1, D_CONF), 0.5),
        'state_conv_b': nrm(ks[3], (DEPTH, DEC_BATCH, SHORT_WIDTH - 1, D_SHORT), 0.5),
        'state_conv_ssm': nrm(ks[4], (DEPTH, DEC_BATCH, SSM_CONV_WIDTH - 1, D_XBC), 1.0),
        'state_ssm': nrm(ks[5], (DEPTH, DEC_BATCH, SSM_HEADS, SSM_HEAD_DIM, SSM_STATE), 0.1),
        'w_in': nrm(ks[6], (DEPTH, D_MODEL, D_IN), D_MODEL ** -0.5),
        'b_gate': nrm(ks[7], (DEPTH, N_BRANCH * D_MODEL), 0.01),
        'conv_a_w': nrm(ks[8], (DEPTH, CONF_WIDTH, D_CONF), CONF_WIDTH ** -0.5),
        'conv_a_b': nrm(ks[9], (DEPTH, D_CONF), 0.01),
        'ln_a_g': 1.0 + nrm(ks[10], (DEPTH, D_CONF), 0.01),
        'ln_a_b': nrm(ks[11], (DEPTH, D_CONF), 0.01),
        'w_out_a': nrm(ks[12], (DEPTH, D_CONF, D_MODEL), BETA * D_CONF ** -0.5),
        'conv_b_w': nrm(ks[13], (DEPTH, SHORT_WIDTH, D_SHORT), SHORT_WIDTH ** -0.5),
        'w_out_b': nrm(ks[14], (DEPTH, D_SHORT, D_MODEL), BETA * D_SHORT ** -0.5),
        'conv_ssm_w': nrm(ks[15], (DEPTH, SSM_CONV_WIDTH, D_XBC), SSM_CONV_WIDTH ** -0.5),
        'conv_ssm_b': nrm(ks[16], (DEPTH, D_XBC), 0.01),
        'a_log': jnp.log(jax.random.uniform(ks[17], (DEPTH, SSM_HEADS), f32, 1.0, 16.0)),
        'dt_bias': dt0 + jnp.log(-jnp.expm1(-dt0)),
        'd_skip': 1.0 + nrm(ks[19], (DEPTH, SSM_HEADS), 0.01),
        'ssm_norm_w': 1.0 + nrm(ks[20], (DEPTH, D_SSM), 0.01),
        'w_out_ssm': nrm(ks[21], (DEPTH, D_SSM, D_MODEL), BETA * D_SSM ** -0.5),
        'w_o': nrm(ks[22], (DEPTH, D_MODEL, D_MODEL), BETA * D_MODEL ** -0.5),
        'ffn1_up': nrm(ks[23], (DEPTH, D_MODEL, 2 * D_FF), D_MODEL ** -0.5),
        'ffn1_down': nrm(ks[24], (DEPTH, D_FF, D_MODEL), BETA * D_FF ** -0.5),
        'ffn2_up': nrm(ks[25], (DEPTH, D_MODEL, 2 * D_FF), D_MODEL ** -0.5),
        'ffn2_down': nrm(ks[26], (DEPTH, D_FF, D_MODEL), BETA * D_FF ** -0.5),
        'ln_g': 1.0 + nrm(ks[27], (DEPTH, 3, D_MODEL), 0.01),
        'ln_b': nrm(ks[28], (DEPTH, 3, D_MODEL), 0.01),
    }


def reference(x_prompt, x_sample, state_conv_a, state_conv_b, state_conv_ssm, state_ssm,
              w_in, b_gate, conv_a_w, conv_a_b, ln_a_g, ln_a_b, w_out_a, conv_b_w, w_out_b,
              conv_ssm_w, conv_ssm_b, a_log, dt_bias, d_skip, ssm_norm_w, w_out_ssm, w_o,
              ffn1_up, ffn1_down, ffn2_up, ffn2_down, ln_g, ln_b):
    w = {'w_in': w_in, 'b_gate': b_gate, 'conv_a_w': conv_a_w, 'conv_a_b': conv_a_b,
         'ln_a_g': ln_a_g, 'ln_a_b': ln_a_b, 'w_out_a': w_out_a, 'conv_b_w': conv_b_w,
         'w_out_b': w_out_b, 'conv_ssm_w': conv_ssm_w, 'conv_ssm_b': conv_ssm_b, 'a_log': a_log,
         'dt_bias': dt_bias, 'd_skip': d_skip, 'ssm_norm_w': ssm_norm_w, 'w_out_ssm': w_out_ssm,
         'w_o': w_o, 'ffn1_up': ffn1_up, 'ffn1_down': ffn1_down, 'ffn2_up': ffn2_up,
         'ffn2_down': ffn2_down, 'ln_g': ln_g, 'ln_b': ln_b}
    bp = x_prompt.shape[0]
    zero_a = jnp.zeros((DEPTH, bp, CONF_WIDTH - 1, D_CONF), x_prompt.dtype)
    zero_b = jnp.zeros((DEPTH, bp, SHORT_WIDTH - 1, D_SHORT), x_prompt.dtype)
    zero_c = jnp.zeros((DEPTH, bp, SSM_CONV_WIDTH - 1, D_XBC), x_prompt.dtype)
    zero_s = jnp.zeros((DEPTH, bp, SSM_HEADS, SSM_HEAD_DIM, SSM_STATE), state_ssm.dtype)
    y_prompt, pa, pb, pc, ps = run_trunk(x_prompt, zero_a, zero_b, zero_c, zero_s, w)
    y_sample, sa, sb, sc, ss = run_trunk(x_sample, state_conv_a, state_conv_b, state_conv_ssm, state_ssm, w)
    return (y_prompt, y_sample, pa, pb, pc, ps, sa, sb, sc, ss)
```

```python
import functools

import jax
import jax.numpy as jnp
from jax import lax
from jax.experimental import pallas as pl
from jax.experimental.pallas import tpu as pltpu

f32 = jnp.float32
bf16 = jnp.bfloat16

D_MODEL = 2048
DEPTH = 4
N_PROMPT, L_PROMPT = 4, 2048
N_SAMPLE, L_SAMPLE = 128, 8
M_PROMPT = N_PROMPT * L_PROMPT
M_SAMPLE = N_SAMPLE * L_SAMPLE
M_TOK = M_PROMPT + M_SAMPLE
D_CONF = 1024
CONF_WIDTH = 31
D_SHORT = 1024
SHORT_WIDTH = 3
D_SSM = 2048
SSM_HEAD_DIM = 64
SSM_HEADS = 32
SSM_GROUPS = 4
HEADS_PER_GROUP = SSM_HEADS // SSM_GROUPS
D_GROUP = D_SSM // SSM_GROUPS
SSM_STATE = 128
SSM_CONV_WIDTH = 4
SSD_CHUNK = 128
D_XBC = D_SSM + 2 * SSM_GROUPS * SSM_STATE
N_BRANCH = 3
D_FF = 5504
ALPHA = (2 * DEPTH) ** 0.25
LN_EPS = 1e-5
RMS_EPS = 1e-5

LANES = 128
SUBLANES = 8
VMEM_LIMIT_BYTES = 56 * 1024 * 1024

D_FF_PAD = 5632
TILE_F = 512
TILE_N = 512
BM_FFN = 512
BM_PROJ = 1024
BM_OUT = 512
DT_TILE = SSM_GROUPS * LANES
PB_BG = 0
PB_Z = D_SHORT
PB_XBC = PB_Z + D_SSM
PB_GATE = PB_XBC + D_XBC
PB_DT = PB_GATE + N_BRANCH * D_MODEL
PB_WIDTH = PB_DT + DT_TILE
CONV_TL = 256
CONV_BB = 16


def _params(n_axes):
    return pltpu.CompilerParams(dimension_semantics=("arbitrary",) * n_axes,
                                vmem_limit_bytes=VMEM_LIMIT_BYTES)


def _dot(a, b):
    return jnp.dot(a, b, preferred_element_type=f32)


def _dot_nt(a, b):
    return lax.dot_general(a, b, (((1,), (1,)), ((), ())), preferred_element_type=f32)


def _sigmoid(v):
    return 1.0 / (1.0 + jnp.exp(-v))


def _silu(v):
    return v * _sigmoid(v)


def _softplus(v):
    return jnp.maximum(v, 0.0) + jnp.log1p(jnp.exp(-jnp.abs(v)))


def _layer_norm(v, g, b):
    mu = jnp.mean(v, axis=-1, keepdims=True)
    d = v - mu
    var = jnp.mean(d * d, axis=-1, keepdims=True)
    return d * lax.rsqrt(var + LN_EPS) * g + b


def _ffn_kernel(x_ref, wg_ref, wu_ref, wd_ref, g_ref, b_ref, o_ref, xb_ref):
    j = pl.program_id(1)

    @pl.when(j == 0)
    def _():
        xb_ref[...] = x_ref[...].astype(bf16)
        o_ref[...] = jnp.zeros_like(o_ref)

    xb = xb_ref[...]
    gate = _dot(xb, wg_ref[...])
    up = _dot(xb, wu_ref[...])
    h = (_silu(gate) * up).astype(bf16)
    for n in range(D_MODEL // TILE_N):
        sl = slice(n * TILE_N, (n + 1) * TILE_N)
        o_ref[:, sl] += _dot(h, wd_ref[:, sl])

    @pl.when(j == pl.num_programs(1) - 1)
    def _():
        v = ALPHA * x_ref[...] + 0.5 * o_ref[...]
        o_ref[...] = _layer_norm(v, g_ref[...], b_ref[...])


def _ffn(x, wg, wu, wd, ln_g, ln_b, layer, which):
    ln_idx = layer * 3 + which
    return pl.pallas_call(
        _ffn_kernel,
        grid=(M_TOK // BM_FFN, D_FF_PAD // TILE_F),
        in_specs=[
            pl.BlockSpec((BM_FFN, D_MODEL), lambda i, j: (i, 0)),
            pl.BlockSpec((None, D_MODEL, TILE_F), lambda i, j: (layer, 0, j)),
            pl.BlockSpec((None, D_MODEL, TILE_F), lambda i, j: (layer, 0, j)),
            pl.BlockSpec((None, TILE_F, D_MODEL), lambda i, j: (layer, j, 0)),
            pl.BlockSpec((None, 1, D_MODEL), lambda i, j: (ln_idx, 0, 0)),
            pl.BlockSpec((None, 1, D_MODEL), lambda i, j: (ln_idx, 0, 0)),
        ],
        out_specs=pl.BlockSpec((BM_FFN, D_MODEL), lambda i, j: (i, 0)),
        out_shape=jax.ShapeDtypeStruct((M_TOK, D_MODEL), f32),
        scratch_shapes=[pltpu.VMEM((BM_FFN, D_MODEL), bf16)],
        compiler_params=_params(2),
        name="ffn",
    )(x, wg, wu, wd, ln_g, ln_b)


def _proj_pair_kernel(x_ref, w1_ref, w2_ref, o_ref, xb_ref, *, n_glu):
    j = pl.program_id(1)

    @pl.when(j == 0)
    def _():
        xb_ref[...] = x_ref[...].astype(bf16)

    xb = xb_ref[...]
    p1 = _dot(xb, w1_ref[...])
    p2 = _dot(xb, w2_ref[...])

    @pl.when(j < n_glu)
    def _():
        o_ref[...] = p1 * _sigmoid(p2)

    @pl.when(j >= n_glu)
    def _():
        o_ref[...] = p1 * p2


def _proj_pair(x, w1, w2, layer):
    n_out = D_CONF + D_SHORT
    return pl.pallas_call(
        functools.partial(_proj_pair_kernel, n_glu=D_CONF // TILE_N),
        grid=(M_TOK // BM_PROJ, n_out // TILE_N),
        in_specs=[
            pl.BlockSpec((BM_PROJ, D_MODEL), lambda i, j: (i, 0)),
            pl.BlockSpec((None, D_MODEL, TILE_N), lambda i, j: (layer, 0, j)),
            pl.BlockSpec((None, D_MODEL, TILE_N), lambda i, j: (layer, 0, j)),
        ],
        out_specs=pl.BlockSpec((BM_PROJ, TILE_N), lambda i, j: (i, j)),
        out_shape=jax.ShapeDtypeStruct((M_TOK, n_out), f32),
        scratch_shapes=[pltpu.VMEM((BM_PROJ, D_MODEL), bf16)],
        compiler_params=_params(2),
        name="proj_pair",
    )(x, w1, w2)


def _proj_main_kernel(x_ref, w_ref, bias_ref, o_ref, xb_ref, *, n_plain, n_sig):
    j = pl.program_id(1)

    @pl.when(j == 0)
    def _():
        xb_ref[...] = x_ref[...].astype(bf16)

    p = _dot(xb_ref[...], w_ref[...])

    @pl.when(j < n_plain)
    def _():
        o_ref[...] = p

    @pl.when(jnp.logical_and(j >= n_plain, j < n_plain + n_sig))
    def _():
        o_ref[...] = _sigmoid(p + bias_ref[...])

    @pl.when(j >= n_plain + n_sig)
    def _():
        o_ref[...] = _softplus(p + bias_ref[...])


def _proj_main(x, w, bias, layer):
    return pl.pallas_call(
        functools.partial(_proj_main_kernel, n_plain=PB_GATE // TILE_N,
                          n_sig=(PB_DT - PB_GATE) // TILE_N),
        grid=(M_TOK // BM_PROJ, PB_WIDTH // TILE_N),
        in_specs=[
            pl.BlockSpec((BM_PROJ, D_MODEL), lambda i, j: (i, 0)),
            pl.BlockSpec((None, D_MODEL, TILE_N), lambda i, j: (layer, 0, j)),
            pl.BlockSpec((None, 1, TILE_N), lambda i, j: (layer, 0, j)),
        ],
        out_specs=pl.BlockSpec((BM_PROJ, TILE_N), lambda i, j: (i, j)),
        out_shape=jax.ShapeDtypeStruct((M_TOK, PB_WIDTH), f32),
        scratch_shapes=[pltpu.VMEM((BM_PROJ, D_MODEL), bf16)],
        compiler_params=_params(2),
        name="proj_main",
    )(x, w, bias)


def _dwconv_kernel(*refs, width, halo, bb, tl, n_l, has_state, mode, n_alias):
    it = iter(refs)
    u_ref = next(it)
    st_ref = next(it) if has_state else None
    w_ref = next(it)
    bias_ref = next(it) if mode in ("a", "c") else None
    if mode == "a":
        lng_ref, lnb_ref = next(it), next(it)
    if mode == "b":
        bg_ref = next(it)
    for _ in range(n_alias):
        next(it)
    o_ref = next(it)
    ns_ref = next(it)
    full = next(it)

    tc = u_ref.shape[-1]
    hist = width - 1
    step = pl.program_id(2)

    @pl.when(step == 0)
    def _():
        if has_state:
            full[:, halo - hist:halo, :] = st_ref[...]
        else:
            full[:, halo - hist:halo, :] = jnp.zeros((bb, hist, tc), f32)

    full[:, halo:halo + tl, :] = u_ref[...].reshape(bb, tl, tc)

    acc = None
    for k in range(width):
        start = halo - hist + k
        term = full[:, start:start + tl, :] * w_ref[k:k + 1, :]
        acc = term if acc is None else acc + term

    if mode == "a":
        res = _silu(_layer_norm(acc + bias_ref[...], lng_ref[...], lnb_ref[...]))
    elif mode == "b":
        res = bg_ref[...].reshape(bb, tl, tc) * acc
    else:
        res = _silu(acc + bias_ref[...])
    o_ref[...] = res.reshape(bb * tl, tc).astype(o_ref.dtype)

    tail = full[:, halo + tl - hist:halo + tl, :]
    ns_ref[...] = tail
    if n_l > 1:
        full[:, halo - hist:halo, :] = tail


def _dwconv(u, u_col, n_ct, w, layer, *, mode, sample, out_dtype, out_width,
            state=None, bias=None, ln=None, bg=None, bg_col=0, out_prev=None, ns_prev=None):
    width = w.shape[1]
    hist = width - 1
    halo = -(-hist // SUBLANES) * SUBLANES
    tc = D_CONF
    if sample:
        bb, tl, n_l = CONV_BB, L_SAMPLE, 1
        n_b = N_SAMPLE // bb
        row0 = M_PROMPT // (bb * tl)
        n_seq = N_SAMPLE
    else:
        bb, tl, n_l = 1, CONV_TL, L_PROMPT // CONV_TL
        n_b = N_PROMPT
        row0 = 0
        n_seq = N_PROMPT
    has_state = state is not None

    def row_map(b, c, l):
        return row0 + b * n_l + l

    args = [u]
    in_specs = [pl.BlockSpec((bb * tl, tc), lambda b, c, l: (row_map(b, c, l), u_col + c))]
    if has_state:
        args.append(state)
        in_specs.append(pl.BlockSpec((None, bb, hist, tc), lambda b, c, l: (layer, b, 0, c)))
    args.append(w)
    in_specs.append(pl.BlockSpec((None, width, tc), lambda b, c, l: (layer, 0, c)))
    if mode in ("a", "c"):
        args.append(bias)
        in_specs.append(pl.BlockSpec((None, 1, tc), lambda b, c, l: (layer, 0, c)))
    if mode == "a":
        for arr in ln:
            args.append(arr)
            in_specs.append(pl.BlockSpec((None, 1, tc), lambda b, c, l: (layer, 0, c)))
    if mode == "b":
        args.append(bg)
        in_specs.append(pl.BlockSpec((bb * tl, tc), lambda b, c, l: (row_map(b, c, l), bg_col + c)))
    aliases = {}
    n_alias = 0
    for out_idx, prev in enumerate((out_prev, ns_prev)):
        if prev is not None:
            aliases[len(args)] = out_idx
            args.append(prev)
            in_specs.append(pl.BlockSpec(memory_space=pl.ANY))
            n_alias += 1

    kern = functools.partial(_dwconv_kernel, width=width, halo=halo, bb=bb, tl=tl, n_l=n_l,
                             has_state=has_state, mode=mode, n_alias=n_alias)
    return pl.pallas_call(
        kern,
        grid=(n_b, n_ct, n_l),
        in_specs=in_specs,
        out_specs=[
            pl.BlockSpec((bb * tl, tc), lambda b, c, l: (row_map(b, c, l), c)),
            pl.BlockSpec((None, bb, hist, tc), lambda b, c, l: (layer, b, 0, c)),
        ],
        out_shape=[
            jax.ShapeDtypeStruct((M_TOK, out_width), out_dtype),
            jax.ShapeDtypeStruct((DEPTH, n_seq, hist, n_ct * tc), f32),
        ],
        scratch_shapes=[pltpu.VMEM((bb, halo + tl, tc), f32)],
        input_output_aliases=aliases,
        compiler_params=_params(3),
        name="dwconv_" + mode + ("_s" if sample else "_p"),
    )(*args)


def _split3(v):
    hi = v.astype(bf16)
    r1 = v - hi.astype(f32)
    mid = r1.astype(bf16)
    lo = (r1 - mid.astype(f32)).astype(bf16)
    return hi, mid, lo


def _ssd_kernel(*refs, qi, n_c, has_init, n_alias):
    it = iter(refs)
    x_ref, bm_ref, cm_ref, dt_ref, z_ref = next(it), next(it), next(it), next(it), next(it)
    alog_ref, dsk_ref, nw_ref = next(it), next(it), next(it)
    init_ref = next(it) if has_init else None
    for _ in range(n_alias):
        next(it)
    y_ref = next(it)
    ns_ref = next(it)
    s_ref = next(it)
    padded = qi < SSD_CHUNK
    if padded:
        xpad, bpad, dtpad = next(it), next(it), next(it)

    q = SSD_CHUNK
    chunk = pl.program_id(2)

    @pl.when(chunk == 0)
    def _():
        if has_init:
            s_ref[...] = init_ref[...].reshape(D_GROUP, SSM_STATE)
        else:
            s_ref[...] = jnp.zeros_like(s_ref)

    if padded:
        @pl.when(jnp.logical_and(jnp.logical_and(pl.program_id(0) == 0, pl.program_id(1) == 0),
                                 chunk == 0))
        def _():
            xpad[...] = jnp.zeros_like(xpad)
            bpad[...] = jnp.zeros_like(bpad)
            dtpad[...] = jnp.zeros_like(dtpad)

        xpad[0:qi, :] = x_ref[...]
        bpad[0:qi, :] = bm_ref[...]
        dtpad[0:qi, :] = dt_ref[...]
        xj, bj, dtj = xpad[...], bpad[...], dtpad[...]
    else:
        xj, bj, dtj = x_ref[...], bm_ref[...], dt_ref[...]
    xi = x_ref[...]
    ci = cm_ref[...].astype(bf16)
    bjb = bj.astype(bf16)

    a_row = -jnp.exp(alog_ref[...])
    dta = dtj * a_row
    row = lax.broadcasted_iota(jnp.int32, (q, q), 0)
    col = lax.broadcasted_iota(jnp.int32, (q, q), 1)
    causal = row >= col
    tril = causal.astype(bf16)
    acum = None
    for part in _split3(dta):
        term = _dot(tril, part)
        acum = term if acum is None else acum + term
    acum_t = acum.T
    a_last = acum[q - 1:q, :]
    dt_decay_end = jnp.exp(a_last - acum) * dtj
    causal_i = causal[0:qi, :]
    acum_i = acum[0:qi, :]

    cb = _dot_nt(ci, bjb)
    s_old = s_ref[...]
    y_off = _dot_nt(ci, s_old.astype(bf16))
    left = lax.broadcasted_iota(jnp.int32, (1, LANES), 1) < SSM_HEAD_DIM
    upper = lax.broadcasted_iota(jnp.int32, (LANES, 1), 0) < SSM_HEAD_DIM

    y_parts, xs_parts, s_parts = [], [], []
    for pair in range(HEADS_PER_GROUP // 2):
        h0, h1 = 2 * pair, 2 * pair + 1
        sl = slice(pair * LANES, (pair + 1) * LANES)
        m_parts = []
        for h in (h0, h1):
            seg = acum_i[:, h:h + 1] - acum_t[h:h + 1, :]
            lmat = jnp.where(causal_i, jnp.exp(jnp.minimum(seg, 0.0)), 0.0)
            m_parts.append((cb * lmat).astype(bf16))
        m_cat = jnp.concatenate(m_parts, axis=1)
        xp = xj[:, sl]
        xdt = xp * jnp.where(left, dtj[:, h0:h0 + 1], dtj[:, h1:h1 + 1])
        x_blk = jnp.concatenate([jnp.where(left, xdt, 0.0), jnp.where(left, 0.0, xdt)],
                                axis=0).astype(bf16)
        y_diag = _dot(m_cat, x_blk)
        e_in = jnp.where(left, jnp.exp(acum_i[:, h0:h0 + 1]), jnp.exp(acum_i[:, h1:h1 + 1]))
        y_parts.append(y_diag + e_in * y_off[:, sl] + dsk_ref[:, sl] * xi[:, sl])
        xs_parts.append(xp * jnp.where(left, dt_decay_end[:, h0:h0 + 1],
                                       dt_decay_end[:, h1:h1 + 1]))
        chunk_decay = jnp.where(upper, jnp.exp(acum_t[h0:h0 + 1, q - 1:q]),
                                jnp.exp(acum_t[h1:h1 + 1, q - 1:q]))
        s_parts.append(s_old[sl, :] * chunk_decay)

    y = jnp.concatenate(y_parts, axis=1)
    z = z_ref[...]
    hg = y * _silu(z)
    ms = jnp.mean(hg * hg, axis=-1, keepdims=True)
    y_ref[...] = (hg * lax.rsqrt(ms + RMS_EPS) * nw_ref[...]).astype(y_ref.dtype)

    xs_t = jnp.concatenate(xs_parts, axis=1).T.astype(bf16)
    s_new = jnp.concatenate(s_parts, axis=0) + _dot(xs_t, bjb)
    s_ref[...] = s_new

    @pl.when(chunk == n_c - 1)
    def _():
        ns_ref[...] = s_new.reshape(HEADS_PER_GROUP, SSM_HEAD_DIM, SSM_STATE)


def _ssd(xact, pb, alog, dskip, norm_w, layer, *, sample, init=None, y_prev=None, ns_prev=None):
    if sample:
        qi, n_c, n_seq = L_SAMPLE, 1, N_SAMPLE
        row0 = M_PROMPT // qi
    else:
        qi, n_c, n_seq = SSD_CHUNK, L_PROMPT // SSD_CHUNK, N_PROMPT
        row0 = 0
    has_init = init is not None

    def rows(b, g, c):
        return row0 + b * n_c + c

    args = [xact, xact, xact, pb, pb, alog, dskip, norm_w]
    in_specs = [
        pl.BlockSpec((qi, D_GROUP), lambda b, g, c: (rows(b, g, c), g)),
        pl.BlockSpec((qi, SSM_STATE), lambda b, g, c: (rows(b, g, c), D_SSM // SSM_STATE + g)),
        pl.BlockSpec((qi, SSM_STATE),
                     lambda b, g, c: (rows(b, g, c), D_SSM // SSM_STATE + SSM_GROUPS + g)),
        pl.BlockSpec((qi, LANES), lambda b, g, c: (rows(b, g, c), PB_DT // LANES + g)),
        pl.BlockSpec((qi, D_GROUP), lambda b, g, c: (rows(b, g, c), PB_Z // D_GROUP + g)),
        pl.BlockSpec((None, 1, LANES), lambda b, g, c: (layer * SSM_GROUPS + g, 0, 0)),
        pl.BlockSpec((None, 1, D_GROUP), lambda b, g, c: (layer * SSM_GROUPS + g, 0, 0)),
        pl.BlockSpec((None, 1, D_GROUP), lambda b, g, c: (layer * SSM_GROUPS + g, 0, 0)),
    ]
    state_block = (None, None, None, HEADS_PER_GROUP, SSM_HEAD_DIM, SSM_STATE)
    if has_init:
        args.append(init)
        in_specs.append(pl.BlockSpec(state_block, lambda b, g, c: (layer, b, g, 0, 0, 0)))
    aliases = {}
    n_alias = 0
    for out_idx, prev in enumerate((y_prev, ns_prev)):
        if prev is not None:
            aliases[len(args)] = out_idx
            args.append(prev)
            in_specs.append(pl.BlockSpec(memory_space=pl.ANY))
            n_alias += 1
    scratch = [pltpu.VMEM((D_GROUP, SSM_STATE), f32)]
    if qi < SSD_CHUNK:
        scratch += [pltpu.VMEM((SSD_CHUNK, D_GROUP), f32), pltpu.VMEM((SSD_CHUNK, SSM_STATE), f32),
                    pltpu.VMEM((SSD_CHUNK, LANES), f32)]
    kern = functools.partial(_ssd_kernel, qi=qi, n_c=n_c, has_init=has_init, n_alias=n_alias)
    return pl.pallas_call(
        kern,
        grid=(n_seq, SSM_GROUPS, n_c),
        in_specs=in_specs,
        out_specs=[
            pl.BlockSpec((qi, D_GROUP), lambda b, g, c: (rows(b, g, c), g)),
            pl.BlockSpec(state_block, lambda b, g, c: (layer, b, g, 0, 0, 0)),
        ],
        out_shape=[
            jax.ShapeDtypeStruct((M_TOK, D_SSM), bf16),
            jax.ShapeDtypeStruct((DEPTH, n_seq, SSM_GROUPS, HEADS_PER_GROUP, SSM_HEAD_DIM, SSM_STATE),
                                 f32),
        ],
        scratch_shapes=scratch,
        input_output_aliases=aliases,
        compiler_params=_params(3),
        name="ssd_s" if sample else "ssd_p",
    )(*args)


def _merge_kernel(fa_ref, fb_ref, fc_ref, wa_ref, wb_ref, wc_ref, g0_ref, g1_ref, g2_ref, o_ref):
    ya = _dot(fa_ref[...], wa_ref[...])
    yb = _dot(fb_ref[...], wb_ref[...])
    yc = _dot(fc_ref[...], wc_ref[...])
    o_ref[...] = (g0_ref[...] * ya + g1_ref[...] * yb + g2_ref[...] * yc).astype(o_ref.dtype)


def _merge(fa, fb, fc, wa, wb, wc, pb, layer):
    def gate_spec(k):
        col0 = (PB_GATE + k * D_MODEL) // TILE_N
        return pl.BlockSpec((BM_PROJ, TILE_N), lambda i, j: (i, col0 + j))

    return pl.pallas_call(
        _merge_kernel,
        grid=(M_TOK // BM_PROJ, D_MODEL // TILE_N),
        in_specs=[
            pl.BlockSpec((BM_PROJ, D_CONF), lambda i, j: (i, 0)),
            pl.BlockSpec((BM_PROJ, D_SHORT), lambda i, j: (i, 0)),
            pl.BlockSpec((BM_PROJ, D_SSM), lambda i, j: (i, 0)),
            pl.BlockSpec((None, D_CONF, TILE_N), lambda i, j: (layer, 0, j)),
            pl.BlockSpec((None, D_SHORT, TILE_N), lambda i, j: (layer, 0, j)),
            pl.BlockSpec((None, D_SSM, TILE_N), lambda i, j: (layer, 0, j)),
            gate_spec(0), gate_spec(1), gate_spec(2),
        ],
        out_specs=pl.BlockSpec((BM_PROJ, TILE_N), lambda i, j: (i, j)),
        out_shape=jax.ShapeDtypeStruct((M_TOK, D_MODEL), bf16),
        compiler_params=_params(2),
        name="merge",
    )(fa, fb, fc, wa, wb, wc, pb, pb, pb)


def _oproj_kernel(m_ref, w_ref, x_ref, g_ref, b_ref, o_ref):
    v = ALPHA * x_ref[...] + _dot(m_ref[...], w_ref[...])
    o_ref[...] = _layer_norm(v, g_ref[...], b_ref[...])


def _oproj(merged, w_o, x, ln_g, ln_b, layer):
    ln_idx = layer * 3 + 1
    return pl.pallas_call(
        _oproj_kernel,
        grid=(M_TOK // BM_OUT,),
        in_specs=[
            pl.BlockSpec((BM_OUT, D_MODEL), lambda i: (i, 0)),
            pl.BlockSpec((None, D_MODEL, D_MODEL), lambda i: (layer, 0, 0)),
            pl.BlockSpec((BM_OUT, D_MODEL), lambda i: (i, 0)),
            pl.BlockSpec((None, 1, D_MODEL), lambda i: (ln_idx, 0, 0)),
            pl.BlockSpec((None, 1, D_MODEL), lambda i: (ln_idx, 0, 0)),
        ],
        out_specs=pl.BlockSpec((BM_OUT, D_MODEL), lambda i: (i, 0)),
        out_shape=jax.ShapeDtypeStruct((M_TOK, D_MODEL), f32),
        compiler_params=_params(1),
        name="oproj",
    )(merged, w_o, x, ln_g, ln_b)


def _prep_ffn(w_up, w_down):
    pad = D_FF_PAD - D_FF
    wg = jnp.pad(w_up[:, :, :D_FF].astype(bf16), ((0, 0), (0, 0), (0, pad)))
    wu = jnp.pad(w_up[:, :, D_FF:].astype(bf16), ((0, 0), (0, 0), (0, pad)))
    wd = jnp.pad(w_down.astype(bf16), ((0, 0), (0, pad), (0, 0)))
    return wg, wu, wd


def _prep_w_in(w_in, b_gate, dt_bias):
    sizes = [D_CONF, D_CONF, D_SHORT, D_SHORT, D_SHORT, D_SSM, D_XBC, SSM_HEADS, N_BRANCH * D_MODEL]
    offs = [0]
    for s in sizes:
        offs.append(offs[-1] + s)
    pa, pa_gate, bg, cg, v, z, xbc, dt, gates = [
        w_in[:, :, offs[k]:offs[k + 1]].astype(bf16) for k in range(len(sizes))]
    w1 = jnp.concatenate([pa, cg], axis=-1)
    w2 = jnp.concatenate([pa_gate, v], axis=-1)
    dt_slots = jnp.pad(dt.reshape(DEPTH, D_MODEL, SSM_GROUPS, HEADS_PER_GROUP),
                       ((0, 0), (0, 0), (0, 0), (0, LANES - HEADS_PER_GROUP)))
    wb = jnp.concatenate([bg, z, xbc, gates, dt_slots.reshape(DEPTH, D_MODEL, DT_TILE)], axis=-1)
    dtb_slots = jnp.pad(dt_bias.reshape(DEPTH, SSM_GROUPS, HEADS_PER_GROUP),
                        ((0, 0), (0, 0), (0, LANES - HEADS_PER_GROUP)))
    bias = jnp.concatenate([jnp.zeros((DEPTH, PB_GATE), f32), b_gate,
                            dtb_slots.reshape(DEPTH, DT_TILE)], axis=-1)
    return w1, w2, wb, bias.reshape(DEPTH, 1, PB_WIDTH)


def kernel(x_prompt, x_sample, state_conv_a, state_conv_b, state_conv_ssm, state_ssm, w_in, b_gate, conv_a_w, conv_a_b, ln_a_g, ln_a_b, w_out_a, conv_b_w, w_out_b, conv_ssm_w, conv_ssm_b, a_log, dt_bias, d_skip, ssm_norm_w, w_out_ssm, w_o, ffn1_up, ffn1_down, ffn2_up, ffn2_down, ln_g, ln_b):
    ffn1 = _prep_ffn(ffn1_up, ffn1_down)
    ffn2 = _prep_ffn(ffn2_up, ffn2_down)
    w1, w2, wb, bias_b = _prep_w_in(w_in, b_gate, dt_bias)
    wa = w_out_a.astype(bf16)
    wbo = w_out_b.astype(bf16)
    wc = w_out_ssm.astype(bf16)
    wo = w_o.astype(bf16)
    ln_g3 = ln_g.reshape(DEPTH * 3, 1, D_MODEL)
    ln_b3 = ln_b.reshape(DEPTH * 3, 1, D_MODEL)
    conv_a_b3 = conv_a_b.reshape(DEPTH, 1, D_CONF)
    ln_a_g3 = ln_a_g.reshape(DEPTH, 1, D_CONF)
    ln_a_b3 = ln_a_b.reshape(DEPTH, 1, D_CONF)
    conv_ssm_b3 = conv_ssm_b.reshape(DEPTH, 1, D_XBC)
    alog = jnp.pad(a_log.reshape(DEPTH * SSM_GROUPS, 1, HEADS_PER_GROUP),
                   ((0, 0), (0, 0), (0, LANES - HEADS_PER_GROUP)))
    dskip = jnp.broadcast_to(d_skip[:, :, None], (DEPTH, SSM_HEADS, SSM_HEAD_DIM)).reshape(
        DEPTH * SSM_GROUPS, 1, D_GROUP)
    norm_w = ssm_norm_w.reshape(DEPTH * SSM_GROUPS, 1, D_GROUP)
    state_ssm_g = state_ssm.reshape(DEPTH, N_SAMPLE, SSM_GROUPS, HEADS_PER_GROUP, SSM_HEAD_DIM,
                                    SSM_STATE)

    x = jnp.concatenate([x_prompt.reshape(M_PROMPT, D_MODEL), x_sample.reshape(M_SAMPLE, D_MODEL)],
                        axis=0)
    nsa_p = nsb_p = nsc_p = nss_p = None
    nsa_s = nsb_s = nsc_s = nss_s = None
    for layer in range(DEPTH):
        x = _ffn(x, *ffn1, ln_g3, ln_b3, layer, 0)
        pa = _proj_pair(x, w1, w2, layer)
        pb = _proj_main(x, wb, bias_b, layer)

        fa, nsa_p = _dwconv(pa, 0, 1, conv_a_w, layer, mode="a", sample=False, out_dtype=bf16,
                            out_width=D_CONF, bias=conv_a_b3, ln=(ln_a_g3, ln_a_b3), ns_prev=nsa_p)
        fa, nsa_s = _dwconv(pa, 0, 1, conv_a_w, layer, mode="a", sample=True, out_dtype=bf16,
                            out_width=D_CONF, state=state_conv_a, bias=conv_a_b3,
                            ln=(ln_a_g3, ln_a_b3), out_prev=fa, ns_prev=nsa_s)
        fb, nsb_p = _dwconv(pa, 1, 1, conv_b_w, layer, mode="b", sample=False, out_dtype=bf16,
                            out_width=D_SHORT, bg=pb, bg_col=PB_BG // D_CONF, ns_prev=nsb_p)
        fb, nsb_s = _dwconv(pa, 1, 1, conv_b_w, layer, mode="b", sample=True, out_dtype=bf16,
                            out_width=D_SHORT, state=state_conv_b, bg=pb, bg_col=PB_BG // D_CONF,
                            out_prev=fb, ns_prev=nsb_s)
        n_ct = D_XBC // D_CONF
        xact, nsc_p = _dwconv(pb, PB_XBC // D_CONF, n_ct, conv_ssm_w, layer, mode="c", sample=False,
                              out_dtype=f32, out_width=D_XBC, bias=conv_ssm_b3, ns_prev=nsc_p)
        xact, nsc_s = _dwconv(pb, PB_XBC // D_CONF, n_ct, conv_ssm_w, layer, mode="c", sample=True,
                              out_dtype=f32, out_width=D_XBC, state=state_conv_ssm, bias=conv_ssm_b3,
                              out_prev=xact, ns_prev=nsc_s)

        fc, nss_p = _ssd(xact, pb, alog, dskip, norm_w, layer, sample=False, ns_prev=nss_p)
        fc, nss_s = _ssd(xact, pb, alog, dskip, norm_w, layer, sample=True, init=state_ssm_g,
                         y_prev=fc, ns_prev=nss_s)

        merged = _merge(fa, fb, fc, wa, wbo, wc, pb, layer)
        x = _oproj(merged, wo, x, ln_g3, ln_b3, layer)
        x = _ffn(x, *ffn2, ln_g3, ln_b3, layer, 2)

    y_prompt = x[:M_PROMPT].reshape(N_PROMPT, L_PROMPT, D_MODEL)
    y_sample = x[M_PROMPT:].reshape(N_SAMPLE, L_SAMPLE, D_MODEL)
    ssm_shape = (SSM_HEADS, SSM_HEAD_DIM, SSM_STATE)
    return (y_prompt, y_sample, nsa_p, nsb_p, nsc_p,
            nss_p.reshape((DEPTH, N_PROMPT) + ssm_shape),
            nsa_s, nsb_s, nsc_s, nss_s.reshape((DEPTH, N_SAMPLE) + ssm_shape))
```
